```python
import jax, jax.numpy as jnp
from jax import lax
import numpy as np

D_MODEL = 4096
BATCH = 4
SEQ = 2048
DEPTH = 2

GRID_W = 64
CTX_LEN = 256
LRU_W = D_MODEL // 4
LRU_BLOCKS = 8
LRU_BW = LRU_W // LRU_BLOCKS
LRU_C = 8.0
CONV_W = 4
NA_DH = 128
NA_W = D_MODEL // 2
NA_HEADS = NA_W // NA_DH
WIN_R = 8
WIN_C = 16
ROPE_BASE = 10000.0
FFT_W = D_MODEL // 4
FFT_GROUPS = 4
FFT_CG = FFT_W // FFT_GROUPS
N_BRANCH = 3
IN_W = LRU_W + 3 * NA_W + FFT_W + N_BRANCH * D_MODEL
IN_SPLITS = [LRU_W, LRU_W + NA_W, LRU_W + 2 * NA_W, LRU_W + 3 * NA_W, LRU_W + 3 * NA_W + FFT_W]
D_FF = 3 * D_MODEL // 2
N_MOD = 9
EPS = 1e-6

kernel_name = "hybrid_lru_natten_fnet_macaron_dit"


def _rmsnorm(x, g):
    xf = x.astype(jnp.float32)
    y = xf * lax.rsqrt(jnp.mean(xf * xf, axis=-1, keepdims=True) + EPS)
    return (y * g.astype(jnp.float32)).astype(x.dtype)


def _modulate(h, shift, scale):
    return h * (1.0 + scale) + shift


def _swiglu(h, w13, w2):
    a, b = jnp.split(h @ w13, 2, axis=-1)
    return (jax.nn.silu(a) * b) @ w2


def _dwconv(u, w, b):
    left = CONV_W // 2
    y = lax.conv_general_dilated(u, w[:, None, :], window_strides=(1,), padding=[(left, CONV_W - 1 - left)],
                                 dimension_numbers=("NWC", "WIO", "NWC"), feature_group_count=u.shape[-1])
    return y + b


def _lin_combine(left, right):
    a1, b1 = left
    a2, b2 = right
    return a1 * a2, a2 * b1 + b2


def _rglru_scan(u, w_a, b_a, w_i, b_i, lam, h0):
    bsz, L, _ = u.shape
    ub = u.reshape(bsz, L, LRU_BLOCKS, LRU_BW)
    r = jax.nn.sigmoid(jnp.einsum("blhi,hij->blhj", ub, w_a).reshape(bsz, L, LRU_W) + b_a)
    i = jax.nn.sigmoid(jnp.einsum("blhi,hij->blhj", ub, w_i).reshape(bsz, L, LRU_W) + b_i)
    log_a = -LRU_C * r * jax.nn.softplus(-lam)
    a = jnp.exp(log_a)
    inp = jnp.sqrt(-jnp.expm1(2.0 * log_a)) * (i * u)
    a_cum, h = lax.associative_scan(_lin_combine, (a, inp), axis=1)
    h = h + a_cum * h0[:, None, :]
    return h, h[:, -1]


def _bidir_rglru(u, uc, conv_w, conv_b, wa, ba, wi, bi, lam, with_ctx_out):
    f32 = jnp.float32
    cw, cb = conv_w.astype(f32), conv_b.astype(f32)
    v_lat = _dwconv(u.astype(f32), cw, cb)
    v_ctx = _dwconv(uc.astype(f32), cw, cb)
    fwd = (wa[0].astype(f32), ba[0].astype(f32), wi[0].astype(f32), bi[0].astype(f32), lam[0].astype(f32))
    bwd = (wa[1].astype(f32), ba[1].astype(f32), wi[1].astype(f32), bi[1].astype(f32), lam[1].astype(f32))
    zeros = jnp.zeros((u.shape[0], LRU_W), f32)
    hc_f, s_f = _rglru_scan(v_ctx, *fwd, zeros)
    hc_b, s_b = _rglru_scan(v_ctx[:, ::-1], *bwd, zeros)
    h_f, _ = _rglru_scan(v_lat, *fwd, s_f)
    h_b, _ = _rglru_scan(v_lat[:, ::-1], *bwd, s_b)
    y = (h_f + h_b[:, ::-1]).astype(u.dtype)
    yc = (hc_f + hc_b[:, ::-1]).astype(u.dtype) if with_ctx_out else None
    return y, yc


def _rope_1d(x, pos):
    d1 = x.shape[-1]
    inv = ROPE_BASE ** (-jnp.arange(0, d1, 2, dtype=jnp.float32) / d1)
    ang = pos[:, None] * inv[None, :]
    cos, sin = jnp.cos(ang)[:, None, :], jnp.sin(ang)[:, None, :]
    x1, x2 = x[..., : d1 // 2], x[..., d1 // 2:]
    return jnp.concatenate([x1 * cos - x2 * sin, x2 * cos + x1 * sin], axis=-1)


def _axial_rope(x):
    L = x.shape[1]
    t = jnp.arange(L)
    row = (t // GRID_W).astype(jnp.float32)
    col = (t % GRID_W).astype(jnp.float32)
    xf = x.astype(jnp.float32)
    half = x.shape[-1] // 2
    return jnp.concatenate([_rope_1d(xf[..., :half], row), _rope_1d(xf[..., half:], col)], axis=-1).astype(x.dtype)


def _neighborhood_attention(q, k, v, k_ctx, v_ctx, rpb):
    bsz, L, H, dh = q.shape
    rows = L // GRID_W
    kr = min(WIN_R, rows)
    scale = dh ** -0.5
    qg = q.reshape(bsz, rows, GRID_W, H, dh).transpose(1, 0, 2, 3, 4)
    kg = k.reshape(bsz, rows, GRID_W, H, dh)
    vg = v.reshape(bsz, rows, GRID_W, H, dh)
    row_ids = jnp.arange(rows)
    row_start = jnp.clip(row_ids - kr // 2, 0, rows - kr)
    cols = jnp.arange(GRID_W)
    col_start = jnp.clip(cols - WIN_C // 2, 0, GRID_W - WIN_C)
    col_in = (cols[None, :] >= col_start[:, None]) & (cols[None, :] < col_start[:, None] + WIN_C)
    col_idx = jnp.clip(cols[None, :] - cols[:, None] + WIN_C - 1, 0, 2 * WIN_C - 2)
    rpb_c = rpb[:, :, col_idx]

    def one_row(args):
        q_r, r, rs = args
        k_blk = lax.dynamic_slice_in_dim(kg, rs, kr, axis=1)
        v_blk = lax.dynamic_slice_in_dim(vg, rs, kr, axis=1)
        row_idx = rs + jnp.arange(kr) - r + WIN_R - 1
        bias = rpb_c[:, row_idx].transpose(0, 2, 1, 3).astype(jnp.float32)
        s_win = jnp.einsum("bqhd,bikhd->bhqik", q_r, k_blk).astype(jnp.float32) * scale + bias[None]
        s_win = jnp.where(col_in[None, None, :, None, :], s_win, -jnp.inf)
        s_ctx = jnp.einsum("bqhd,bkhd->bhqk", q_r, k_ctx).astype(jnp.float32) * scale
        s = jnp.concatenate([s_win.reshape(bsz, H, GRID_W, kr * GRID_W), s_ctx], axis=-1)
        p = jax.nn.softmax(s, axis=-1).astype(q.dtype)
        p_win = p[..., : kr * GRID_W].reshape(bsz, H, GRID_W, kr, GRID_W)
        p_ctx = p[..., kr * GRID_W:]
        return (jnp.einsum("bhqik,bikhd->bqhd", p_win, v_blk)
                + jnp.einsum("bhqk,bkhd->bqhd", p_ctx, v_ctx))

    out = lax.map(one_row, (qg, row_ids, row_start))
    return out.transpose(1, 0, 2, 3, 4).reshape(bsz, L, H * dh)


def _context_attention(q, k, v):
    bsz, Lc, H, dh = q.shape
    s = jnp.einsum("bqhd,bkhd->bhqk", q, k).astype(jnp.float32) * dh ** -0.5
    p = jax.nn.softmax(s, axis=-1).astype(q.dtype)
    return jnp.einsum("bhqk,bkhd->bqhd", p, v).reshape(bsz, Lc, H * dh)


def _fourier_mix(f):
    bsz, L, _ = f.shape
    fg = f.astype(jnp.float32).reshape(bsz, L, FFT_GROUPS, FFT_CG)
    y = jnp.fft.fft2(fg, axes=(1, 3), norm="ortho").real
    return y.reshape(bsz, L, FFT_W).astype(f.dtype)


def _merge(g, y_lru, y_na, y_fft, w_out_lru, w_out_na, w_out_fft, w_o):
    ga, gb, gc = jnp.split(jax.nn.sigmoid(g.astype(jnp.float32)).astype(g.dtype), N_BRANCH, axis=-1)
    m = ga * (y_lru @ w_out_lru) + gb * (y_na @ w_out_na) + gc * (y_fft @ w_out_fft)
    return m @ w_o


def _token_mixer(h, hc, w_in, conv_w, conv_b, lru_wa, lru_ba, lru_wi, lru_bi, lru_lam, rpb,
                 w_out_lru, w_out_na, w_out_fft, w_o, with_ctx_out):
    u, q, k, v, f, g = jnp.split(h @ w_in, IN_SPLITS, axis=-1)
    uc, qc, kc, vc, fc, gc = jnp.split(hc @ w_in, IN_SPLITS, axis=-1)

    def heads(t):
        return t.reshape(t.shape[0], t.shape[1], NA_HEADS, NA_DH)

    y_lru, y_lru_c = _bidir_rglru(u, uc, conv_w, conv_b, lru_wa, lru_ba, lru_wi, lru_bi, lru_lam, with_ctx_out)
    kc_h, vc_h = heads(kc), heads(vc)
    y_na = _neighborhood_attention(_axial_rope(heads(q)), _axial_rope(heads(k)), heads(v), kc_h, vc_h, rpb)
    y = _merge(g, y_lru, y_na, _fourier_mix(f), w_out_lru, w_out_na, w_out_fft, w_o)
    if not with_ctx_out:
        return y, None
    y_na_c = _context_attention(heads(qc), kc_h, vc_h)
    yc = _merge(gc, y_lru_c, y_na_c, _fourier_mix(fc), w_out_lru, w_out_na, w_out_fft, w_o)
    return y, yc


def setup_inputs(seed: int = 0) -> dict:
    key = jax.random.key(seed)
    ks = jax.random.split(key, 25)
    f32 = jnp.float32
    D = D_MODEL

    def nrm(k, shape, s):
        return jax.random.normal(k, shape, f32) * s

    u = jax.random.uniform(ks[17], (DEPTH, 2, LRU_W), f32, minval=0.9, maxval=0.999)
    sa = u ** (1.0 / LRU_C)
    lam = jnp.log(sa) - jnp.log1p(-sa)
    return {
        "x": nrm(ks[0], (BATCH, SEQ, D), 1.0),
        "c": nrm(ks[1], (BATCH, D), 1.0),
        "ctx": nrm(ks[2], (BATCH, CTX_LEN, D), 1.0),
        "c_ctx": nrm(ks[3], (D,), 1.0),
        "w_ada": nrm(ks[4], (DEPTH, D, N_MOD * D), 0.5 * D ** -0.5),
        "b_ada": nrm(ks[5], (DEPTH, N_MOD * D), 0.01),
        "norm_g": 1.0 + nrm(ks[6], (DEPTH, 3, D), 0.02),
        "ffn1_w13": nrm(ks[7], (DEPTH, D, 2 * D_FF), D ** -0.5),
        "ffn1_w2": nrm(ks[8], (DEPTH, D_FF, D), D_FF ** -0.5),
        "ffn2_w13": nrm(ks[9], (DEPTH, D, 2 * D_FF), D ** -0.5),
        "ffn2_w2": nrm(ks[10], (DEPTH, D_FF, D), D_FF ** -0.5),
        "w_in": nrm(ks[11], (DEPTH, D, IN_W), D ** -0.5),
        "conv_w": nrm(ks[12], (DEPTH, CONV_W, LRU_W), CONV_W ** -0.5),
        "conv_b": nrm(ks[13], (DEPTH, LRU_W), 0.01),
        "lru_wa": nrm(ks[14], (DEPTH, 2, LRU_BLOCKS, LRU_BW, LRU_BW), LRU_BW ** -0.5),
        "lru_ba": nrm(ks[15], (DEPTH, 2, LRU_W), 0.01),
        "lru_wi": nrm(ks[16], (DEPTH, 2, LRU_BLOCKS, LRU_BW, LRU_BW), LRU_BW ** -0.5),
        "lru_bi": nrm(ks[18], (DEPTH, 2, LRU_W), 0.01),
        "lru_lam": lam,
        "na_rpb": nrm(ks[19], (DEPTH, NA_HEADS, 2 * WIN_R - 1, 2 * WIN_C - 1), 0.1),
        "w_out_lru": nrm(ks[20], (DEPTH, LRU_W, D), LRU_W ** -0.5),
        "w_out_na": nrm(ks[21], (DEPTH, NA_W, D), NA_W ** -0.5),
        "w_out_fft": nrm(ks[22], (DEPTH, FFT_W, D), FFT_W ** -0.5),
        "w_o": nrm(ks[23], (DEPTH, D, D), D ** -0.5),
        "final_g": 1.0 + nrm(ks[24], (D,), 0.02),
    }


def reference(x, c, ctx, c_ctx, w_ada, b_ada, norm_g, ffn1_w13, ffn1_w2, ffn2_w13, ffn2_w2, w_in, conv_w, conv_b,
              lru_wa, lru_ba, lru_wi, lru_bi, lru_lam, na_rpb, w_out_lru, w_out_na, w_out_fft, w_o, final_g):
    bsz, L, D = x.shape
    xc = ctx
    for l in range(DEPTH):
        last = l == DEPTH - 1
        ml = (jax.nn.silu(c) @ w_ada[l] + b_ada[l]).reshape(bsz, N_MOD, 1, D)
        mc = (jax.nn.silu(c_ctx) @ w_ada[l] + b_ada[l]).reshape(N_MOD, D)
        x = x + 0.5 * ml[:, 2] * _swiglu(_modulate(_rmsnorm(x, norm_g[l, 0]), ml[:, 0], ml[:, 1]),
                                         ffn1_w13[l], ffn1_w2[l])
        xc = xc + 0.5 * mc[2] * _swiglu(_modulate(_rmsnorm(xc, norm_g[l, 0]), mc[0], mc[1]),
                                        ffn1_w13[l], ffn1_w2[l])
        y, yc = _token_mixer(_modulate(_rmsnorm(x, norm_g[l, 1]), ml[:, 3], ml[:, 4]),
                             _modulate(_rmsnorm(xc, norm_g[l, 1]), mc[3], mc[4]),
                             w_in[l], conv_w[l], conv_b[l], lru_wa[l], lru_ba[l], lru_wi[l], lru_bi[l], lru_lam[l],
                             na_rpb[l], w_out_lru[l], w_out_na[l], w_out_fft[l], w_o[l], not last)
        x = x + ml[:, 5] * y
        x = x + 0.5 * ml[:, 8] * _swiglu(_modulate(_rmsnorm(x, norm_g[l, 2]), ml[:, 6], ml[:, 7]),
                                         ffn2_w13[l], ffn2_w2[l])
        if not last:
            xc = xc + mc[5] * yc
            xc = xc + 0.5 * mc[8] * _swiglu(_modulate(_rmsnorm(xc, norm_g[l, 2]), mc[6], mc[7]),
                                            ffn2_w13[l], ffn2_w2[l])
    return _rmsnorm(x, final_g)
```

```python
import functools
import math

import numpy as np
import jax
import jax.numpy as jnp
from jax import lax
from jax.experimental import pallas as pl
from jax.experimental.pallas import tpu as pltpu

D = 4096
B = 4
L = 2048
DEPTH = 2
GRID_W = 64
GRID_H = L // GRID_W
LC = 256
LRU_W = D // 4
LRU_BW = 128
LRU_C = 8.0
CONV_W = 4
CONV_LEFT = CONV_W // 2
NA_DH = 128
NA_W = D // 2
NA_HEADS = NA_W // NA_DH
WIN_R = 8
WIN_C = 16
ROPE_BASE = 10000.0
FFT_W = D // 4
FFT_GROUPS = 4
FFT_CG = FFT_W // FFT_GROUPS
D_FF = 3 * D // 2
N_MOD = 9
EPS = 1e-6
T_LAT = B * L
T_CTX = B * LC
T_ALL = T_LAT + T_CTX
COL_U = 0
COL_Q = LRU_W
COL_V = LRU_W + 2 * NA_W
COL_G = LRU_W + 3 * NA_W + FFT_W

V7X_VMEM_LIMIT_BYTES = 56 * 1024 * 1024
SUBLANES = 8
LANES = 128
MOD_ROWS = 8

BF16 = jnp.bfloat16
F32 = jnp.float32


def _cparams(n_axes):
    return pltpu.CompilerParams(dimension_semantics=("arbitrary",) * n_axes,
                                vmem_limit_bytes=V7X_VMEM_LIMIT_BYTES)


def _row_group(row_tile, tile_rows):
    return jnp.minimum((row_tile * tile_rows) // L, B)


def _dot(a, b):
    return jnp.dot(a, b, preferred_element_type=F32)


def _dot_nt(a, b):
    return lax.dot_general(a, b, (((1,), (1,)), ((), ())), preferred_element_type=F32)


ADA_TN = 1024


def _ada_kernel(c_ref, w_ref, b_ref, o_ref):
    c = c_ref[...]
    s = (c * jax.nn.sigmoid(c)).astype(BF16)
    o_ref[...] = _dot(s, w_ref[...].astype(BF16)) + b_ref[...]


def _ada(c8, w_ada, b_ada):
    n = N_MOD * D
    return pl.pallas_call(
        _ada_kernel,
        out_shape=jax.ShapeDtypeStruct((DEPTH, MOD_ROWS, n), F32),
        grid=(DEPTH, n // ADA_TN),
        in_specs=[pl.BlockSpec((MOD_ROWS, D), lambda l, j: (0, 0)),
                  pl.BlockSpec((None, D, ADA_TN), lambda l, j: (l, 0, j)),
                  pl.BlockSpec((None, 1, ADA_TN), lambda l, j: (l, 0, j))],
        out_specs=pl.BlockSpec((None, MOD_ROWS, ADA_TN), lambda l, j: (l, 0, j)),
        compiler_params=_cparams(2), name="ada")(c8, w_ada, b_ada.reshape(DEPTH, 1, n))


NORM_TR = 256


def _norm_mod_kernel(x_ref, g_ref, shift_ref, scale_ref, o_ref):
    x = x_ref[...]
    y = x * lax.rsqrt(jnp.mean(x * x, axis=-1, keepdims=True) + EPS) * g_ref[...]
    o_ref[...] = (y * (1.0 + scale_ref[...]) + shift_ref[...]).astype(BF16)


def _norm_mod(xa, norm_g3, mod3, l, k_norm, k_shift, k_scale, n_rows):
    ni = n_rows // NORM_TR
    mod_spec = lambda k: pl.BlockSpec(
        (None, 1, D), lambda i: (l * MOD_ROWS + _row_group(i, NORM_TR), 0, k))
    return pl.pallas_call(
        _norm_mod_kernel,
        out_shape=jax.ShapeDtypeStruct((T_ALL, D), BF16),
        grid=(ni,),
        in_specs=[pl.BlockSpec((NORM_TR, D), lambda i: (i, 0)),
                  pl.BlockSpec((None, 1, D), lambda i: (l * 3 + k_norm, 0, 0)),
                  mod_spec(k_shift), mod_spec(k_scale)],
        out_specs=pl.BlockSpec((NORM_TR, D), lambda i: (i, 0)),
        compiler_params=_cparams(1), name="norm_mod")(xa, norm_g3, mod3, mod3)


def _final_norm_kernel(x_ref, g_ref, o_ref):
    x = x_ref[...]
    o_ref[...] = x * lax.rsqrt(jnp.mean(x * x, axis=-1, keepdims=True) + EPS) * g_ref[...]


def _final_norm(xa, final_g):
    return pl.pallas_call(
        _final_norm_kernel,
        out_shape=jax.ShapeDtypeStruct((T_LAT, D), F32),
        grid=(T_LAT // NORM_TR,),
        in_specs=[pl.BlockSpec((NORM_TR, D), lambda i: (i, 0)),
                  pl.BlockSpec((1, D), lambda i: (0, 0))],
        out_specs=pl.BlockSpec((NORM_TR, D), lambda i: (i, 0)),
        compiler_params=_cparams(1), name="final_norm")(xa, final_g.reshape(1, D))


def _mm_kernel(*refs, n_x, n_w, n_aux, n_out, body):
    x_refs = refs[:n_x]
    w_refs = refs[n_x:n_x + n_w]
    aux_refs = refs[n_x + n_w:n_x + n_w + n_aux]
    out_refs = refs[n_x + n_w + n_aux:n_x + n_w + n_aux + n_out]
    wbf_refs = refs[n_x + n_w + n_aux + n_out:]

    @pl.when(pl.program_id(1) == 0)
    def _():
        for w_ref, wbf_ref in zip(w_refs, wbf_refs):
            wbf_ref[...] = w_ref[...].astype(BF16)

    body(x_refs, wbf_refs, aux_refs, out_refs)


def _mm(name, body, *, xs, ws, auxs, outs, tm, tn, ni, nj, i_off=0, aliases=None):
    in_specs, args, scratch = [], [], []
    for x in xs:
        in_specs.append(pl.BlockSpec((tm, x.shape[1]), lambda j, i: (i + i_off, 0)))
        args.append(x)
    for w, layer, col0 in ws:
        k = w.shape[1]
        in_specs.append(pl.BlockSpec((None, k, tn), functools.partial(
            lambda j, i, layer, cb: (layer, 0, cb + j), layer=layer, cb=col0 // tn)))
        args.append(w)
        scratch.append(pltpu.VMEM((k, tn), BF16))
    for a, spec in auxs:
        in_specs.append(spec)
        args.append(a)
    kern = functools.partial(_mm_kernel, n_x=len(xs), n_w=len(ws), n_aux=len(auxs), n_out=len(outs),
                             body=body)
    res = pl.pallas_call(
        kern,
        out_shape=[o for o, _ in outs],
        grid=(nj, ni),
        in_specs=in_specs,
        out_specs=[s for _, s in outs],
        scratch_shapes=scratch,
        input_output_aliases=aliases or {},
        compiler_params=_cparams(2), name=name)(*args)
    return res


def _mod_tile_spec(l, k_mod, tm, tn, i_off):
    return pl.BlockSpec((None, 1, tn), lambda j, i: (
        l * MOD_ROWS + _row_group(i + i_off, tm), 0, k_mod * (D // tn) + j))


UP_TM, UP_TN = 1024, 256


def _swiglu_body(x_refs, w_refs, aux_refs, out_refs):
    h = x_refs[0][...]
    a = _dot(h, w_refs[0][...])
    b = _dot(h, w_refs[1][...])
    out_refs[0][...] = (a * jax.nn.sigmoid(a) * b).astype(BF16)


def _swiglu_up(h, w13, l, n_rows):
    out = jax.ShapeDtypeStruct((T_ALL, D_FF), BF16)
    return _mm("ffn_up", _swiglu_body, xs=[h], ws=[(w13, l, 0), (w13, l, D_FF)], auxs=[],
               outs=[(out, pl.BlockSpec((UP_TM, UP_TN), lambda j, i: (i, j)))],
               tm=UP_TM, tn=UP_TN, ni=n_rows // UP_TM, nj=D_FF // UP_TN)[0]


RES_TN = 512


def _resid_body(x_refs, w_refs, aux_refs, out_refs, *, coef):
    y = _dot(x_refs[0][...], w_refs[0][...])
    out_refs[0][...] = aux_refs[0][...] + (coef * aux_refs[1][...]) * y


def _resid(name, u, w, l, xa, mod3, k_gate, coef, *, tm, row0, n_rows):
    i_off = row0 // tm
    n_x_w = 2
    tile = pl.BlockSpec((tm, RES_TN), lambda j, i: (i + i_off, j))
    xs_spec_rows = u.shape[0]
    assert xs_spec_rows in (T_ALL, n_rows)
    u_off = i_off if xs_spec_rows == T_ALL else 0
    in_x = pl.BlockSpec((tm, u.shape[1]), lambda j, i: (i + u_off, 0))
    body = functools.partial(_resid_body, coef=coef)
    k = w.shape[1]
    kern = functools.partial(_mm_kernel, n_x=1, n_w=1, n_aux=2, n_out=1, body=body)
    return pl.pallas_call(
        kern,
        out_shape=jax.ShapeDtypeStruct((T_ALL, D), F32),
        grid=(D // RES_TN, n_rows // tm),
        in_specs=[in_x,
                  pl.BlockSpec((None, k, RES_TN), lambda j, i: (l, 0, j)),
                  tile,
                  _mod_tile_spec(l, k_gate, tm, RES_TN, i_off)],
        out_specs=tile,
        scratch_shapes=[pltpu.VMEM((k, RES_TN), BF16)],
        input_output_aliases={n_x_w: 0},
        compiler_params=_cparams(2), name=name)(u, w, xa, mod3)


IN_TM, IN_TN = 1024, 512


def _plain_body(x_refs, w_refs, aux_refs, out_refs):
    out_refs[0][...] = _dot(x_refs[0][...], w_refs[0][...]).astype(out_refs[0].dtype)


def _sigmoid_body(x_refs, w_refs, aux_refs, out_refs):
    out_refs[0][...] = jax.nn.sigmoid(_dot(x_refs[0][...], w_refs[0][...]))


def _rope_body(x_refs, w_refs, aux_refs, out_refs):
    acc = _dot(x_refs[0][...], w_refs[0][...])
    cos = aux_refs[0][...]
    sin = aux_refs[1][...]
    lane = lax.broadcasted_iota(jnp.int32, (IN_TM, NA_DH), 1)
    first_half = (lane % (NA_DH // 2)) < (NA_DH // 4)
    for hh in range(IN_TN // NA_DH):
        xh = acc[:, hh * NA_DH:(hh + 1) * NA_DH]
        partner = jnp.where(first_half,
                            pltpu.roll(xh, NA_DH - NA_DH // 4, 1),
                            pltpu.roll(xh, NA_DH // 4, 1))
        out_refs[0][:, hh * NA_DH:(hh + 1) * NA_DH] = (xh * cos + partner * sin).astype(BF16)


def _rope_tables():
    half = NA_DH // 2
    t = np.arange(L)
    inv = (ROPE_BASE ** (-np.arange(0, half, 2, dtype=np.float32) / np.float32(half))).astype(np.float32)
    ang_r = ((t // GRID_W).astype(np.float32)[:, None] * inv[None, :]).astype(np.float32)
    ang_c = ((t % GRID_W).astype(np.float32)[:, None] * inv[None, :]).astype(np.float32)
    cos = np.concatenate([np.cos(ang_r), np.cos(ang_r), np.cos(ang_c), np.cos(ang_c)], axis=1)
    sin = np.concatenate([-np.sin(ang_r), np.sin(ang_r), -np.sin(ang_c), np.sin(ang_c)], axis=1)
    cos = np.concatenate([cos, np.ones((IN_TM, NA_DH))], axis=0).astype(np.float32)
    sin = np.concatenate([sin, np.zeros((IN_TM, NA_DH))], axis=0).astype(np.float32)
    return cos, sin


def _in_proj(name, body, h, w_in, l, col0, width, dtype, n_rows, auxs=()):
    out = jax.ShapeDtypeStruct((T_ALL, width), dtype)
    return _mm(name, body, xs=[h], ws=[(w_in, l, col0)], auxs=list(auxs),
               outs=[(out, pl.BlockSpec((IN_TM, IN_TN), lambda j, i: (i, j)))],
               tm=IN_TM, tn=IN_TN, ni=n_rows // IN_TM, nj=width // IN_TN)[0]


def _rope_aux(cos, sin):
    lat_tiles = T_LAT // IN_TM
    per_seq = L // IN_TM
    spec = pl.BlockSpec((IN_TM, NA_DH), lambda j, i: (jnp.where(i < lat_tiles, i % per_seq, per_seq), 0))
    return [(cos, spec), (sin, spec)]


MERGE_TM, MERGE_TN = 512, 512


def _merge_body(x_refs, w_refs, aux_refs, out_refs):
    m = aux_refs[0][...] * _dot(x_refs[0][...], w_refs[0][...])
    m = m + aux_refs[1][...] * _dot(x_refs[1][...], w_refs[1][...])
    m = m + aux_refs[2][...] * _dot(x_refs[2][...], w_refs[2][...])
    out_refs[0][...] = m.astype(BF16)


def _merge(y_lru, y_na, y_fft, gates, w_out_lru, w_out_na, w_out_fft, l, *, row0, n_rows):
    tm = min(MERGE_TM, n_rows)
    i_off = row0 // tm
    gate_spec = lambda b: pl.BlockSpec((tm, MERGE_TN), lambda j, i: (i + i_off, b * (D // MERGE_TN) + j))
    out = jax.ShapeDtypeStruct((n_rows, D), BF16)
    return _mm("merge", _merge_body, xs=[y_lru, y_na, y_fft],
               ws=[(w_out_lru, l, 0), (w_out_na, l, 0), (w_out_fft, l, 0)],
               auxs=[(gates, gate_spec(0)), (gates, gate_spec(1)), (gates, gate_spec(2))],
               outs=[(out, pl.BlockSpec((tm, MERGE_TN), lambda j, i: (i, j)))],
               tm=tm, tn=MERGE_TN, ni=n_rows // tm, nj=D // MERGE_TN)[0]


LRU_CT = 256
LRU_CHUNK = 256
LRU_PAD = SUBLANES
LRU_T = LC + L


def _softplus(x):
    return jnp.maximum(x, 0.0) + jnp.log1p(jnp.exp(-jnp.abs(x)))


def _lru_kernel(ul_ref, uc_ref, cw_ref, cb_ref, wa_ref, wi_ref, ba_ref, bi_ref, lam_ref, *rest, with_ctx_out):
    if with_ctx_out:
        yl_ref, yc_ref, pad_l, pad_c, a_f, x_f, a_b, x_b = rest
    else:
        yl_ref, pad_l, pad_c, a_f, x_f, a_b, x_b = rest
        yc_ref = None
    zeros_pad = jnp.zeros((LRU_PAD, LRU_CT), F32)
    for pad, src, n in ((pad_l, ul_ref, L), (pad_c, uc_ref, LC)):
        pad[0:LRU_PAD, :] = zeros_pad
        pad[LRU_PAD + n:2 * LRU_PAD + n, :] = zeros_pad
        pad[LRU_PAD:LRU_PAD + n, :] = src[...]

    cw = cw_ref[...]
    cb = cb_ref[...]
    sp = _softplus(-lam_ref[...])
    row = lax.broadcasted_iota(jnp.int32, (LRU_CHUNK, LRU_BW), 0) % SUBLANES

    def gate_terms(v, vb, blk, d):
        cs = slice(blk * LRU_BW, (blk + 1) * LRU_BW)
        r = jax.nn.sigmoid(_dot(vb, wa_ref[d, blk].astype(BF16)) + ba_ref[d:d + 1, cs])
        g = jax.nn.sigmoid(_dot(vb, wi_ref[d, blk].astype(BF16)) + bi_ref[d:d + 1, cs])
        log_a = (-LRU_C) * r * sp[d:d + 1, cs]
        a = jnp.exp(log_a)
        return a, jnp.sqrt(1.0 - a * a) * (g * v)

    def local_scan(a, x, forward):
        s = 1
        while s < SUBLANES:
            if forward:
                m = row >= s
                a_n, x_n = pltpu.roll(a, s, 0), pltpu.roll(x, s, 0)
            else:
                m = row < SUBLANES - s
                a_n, x_n = pltpu.roll(a, LRU_CHUNK - s, 0), pltpu.roll(x, LRU_CHUNK - s, 0)
            x = jnp.where(m, a * x_n + x, x)
            a = jnp.where(m, a * a_n, a)
            s *= 2
        return a, x

    for pad, n, f_off, b_off in ((pad_c, LC, 0, L), (pad_l, L, LC, 0)):
        for r0 in range(0, n, LRU_CHUNK):
            v = cb
            for j in range(CONV_W):
                start = LRU_PAD + r0 + j - CONV_LEFT
                v = v + cw[j:j + 1, :] * pad[start:start + LRU_CHUNK, :]
            for blk in range(LRU_CT // LRU_BW):
                cs = slice(blk * LRU_BW, (blk + 1) * LRU_BW)
                vblk = v[:, cs]
                vb = vblk.astype(BF16)
                a, x = local_scan(*gate_terms(vblk, vb, blk, 0), True)
                a_f[f_off + r0:f_off + r0 + LRU_CHUNK, cs] = a
                x_f[f_off + r0:f_off + r0 + LRU_CHUNK, cs] = x
                a, x = local_scan(*gate_terms(vblk, vb, blk, 1), False)
                a_b[b_off + r0:b_off + r0 + LRU_CHUNK, cs] = a
                x_b[b_off + r0:b_off + r0 + LRU_CHUNK, cs] = x

    n_groups = LRU_T // SUBLANES

    def carry_step(g, carry):
        hf, hb = carry
        rf = pl.multiple_of(g * SUBLANES, SUBLANES)
        h = a_f[pl.ds(rf, SUBLANES), :] * hf + x_f[pl.ds(rf, SUBLANES), :]
        x_f[pl.ds(rf, SUBLANES), :] = h
        hf = h[SUBLANES - 1:SUBLANES, :]
        rb = pl.multiple_of((n_groups - 1 - g) * SUBLANES, SUBLANES)
        h = a_b[pl.ds(rb, SUBLANES), :] * hb + x_b[pl.ds(rb, SUBLANES), :]
        x_b[pl.ds(rb, SUBLANES), :] = h
        hb = h[0:1, :]
        return hf, hb

    zero_state = jnp.zeros((1, LRU_CT), F32)
    lax.fori_loop(0, n_groups, carry_step, (zero_state, zero_state))

    yl_ref[...] = (x_f[LC:LC + L, :] + x_b[0:L, :]).astype(BF16)
    if with_ctx_out:
        yc_ref[...] = (x_f[0:LC, :] + x_b[L:L + LC, :]).astype(BF16)


def _lru(u_all, conv_w, conv_b, lru_wa, lru_wi, lru_ba, lru_bi, lru_lam, l, with_ctx_out):
    nct = LRU_W // LRU_CT
    bpc = LRU_CT // LRU_BW
    vec = lambda rows: pl.BlockSpec((None, rows, LRU_CT), lambda b, c: (l, 0, c))
    wspec = pl.BlockSpec((None, 2, bpc, LRU_BW, LRU_BW), lambda b, c: (l, 0, c, 0, 0))
    out_shape = [jax.ShapeDtypeStruct((T_LAT, LRU_W), BF16)]
    out_specs = [pl.BlockSpec((L, LRU_CT), lambda b, c: (b, c))]
    if with_ctx_out:
        out_shape.append(jax.ShapeDtypeStruct((T_CTX, LRU_W), BF16))
        out_specs.append(pl.BlockSpec((LC, LRU_CT), lambda b, c: (b, c)))
    res = pl.pallas_call(
        functools.partial(_lru_kernel, with_ctx_out=with_ctx_out),
        out_shape=out_shape,
        grid=(B, nct),
        in_specs=[pl.BlockSpec((L, LRU_CT), lambda b, c: (b, c)),
                  pl.BlockSpec((LC, LRU_CT), lambda b, c: (T_LAT // LC + b, c)),
                  vec(CONV_W), vec(1), wspec, wspec, vec(2), vec(2), vec(2)],
        out_specs=out_specs,
        scratch_shapes=[pltpu.VMEM((L + 2 * LRU_PAD, LRU_CT), F32),
                        pltpu.VMEM((LC + 2 * LRU_PAD, LRU_CT), F32)]
                       + [pltpu.VMEM((LRU_T, LRU_CT), F32)] * 4,
        compiler_params=_cparams(2), name="rglru")(
            u_all, u_all, conv_w, conv_b.reshape(DEPTH, 1, LRU_W), lru_wa, lru_wi, lru_ba, lru_bi, lru_lam)
    return res if with_ctx_out else (res[0], None)


KR = min(WIN_R, GRID_H)
N_BASE = WIN_R
NA_SCALE = NA_DH ** -0.5


def _bias_kernel(rpb_ref, o_ref):
    q = lax.broadcasted_iota(jnp.int32, (GRID_W, LANES), 0)
    lane = lax.broadcasted_iota(jnp.int32, (GRID_W, LANES), 1)
    kcol = lane % GRID_W
    col_start = jnp.clip(q - WIN_C // 2, 0, GRID_W - WIN_C)
    col_in = (kcol >= col_start) & (kcol < col_start + WIN_C)
    tiles = []
    for ri in range(2 * WIN_R - 1):
        rowv = jnp.broadcast_to(rpb_ref[ri:ri + 1, :], (GRID_W, LANES))
        lo = pltpu.roll(rowv, LANES - (WIN_C - 1), 1, stride=1, stride_axis=0)
        hi = pltpu.roll(rowv, GRID_W - (WIN_C - 1), 1, stride=1, stride_axis=0)
        t = jnp.where(lane < GRID_W, lo, hi)
        tiles.append(jnp.where(col_in, t, -jnp.inf))
    for base in range(N_BASE):
        for p in range(KR // 2):
            t = jnp.where(lane < GRID_W, tiles[base + 2 * p], tiles[base + 2 * p + 1])
            o_ref[base, :, p * LANES:(p + 1) * LANES] = t


def _bias_tables(na_rpb, l):
    pad = LANES - (2 * WIN_C - 1)
    rpb = jnp.pad(na_rpb, ((0, 0), (0, 0), (0, 0), (0, pad)))
    return pl.pallas_call(
        _bias_kernel,
        out_shape=jax.ShapeDtypeStruct((NA_HEADS, N_BASE, GRID_W, KR * GRID_W), F32),
        grid=(NA_HEADS,),
        in_specs=[pl.BlockSpec((None, None, 2 * WIN_R - 1, LANES), lambda h: (l, h, 0, 0))],
        out_specs=pl.BlockSpec((None, N_BASE, GRID_W, KR * GRID_W), lambda h: (h, 0, 0, 0)),
        compiler_params=_cparams(1), name="na_bias")(rpb)


def _natten_kernel(q_ref, k_ref, v_ref, kc_ref, vc_ref, bias_ref, *rest, with_ctx_out):
    if with_ctx_out:
        qc_ref, o_ref, oc_ref = rest
    else:
        (o_ref,) = rest
    kc = kc_ref[...]
    vc = vc_ref[...]

    def one_row(r, _):
        rs = jnp.clip(r - KR // 2, 0, GRID_H - KR)
        base = rs - r + WIN_R - 1
        q0 = pl.multiple_of(r * GRID_W, GRID_W)
        k0 = pl.multiple_of(rs * GRID_W, GRID_W)
        qr = q_ref[pl.ds(q0, GRID_W), :]
        s_win = _dot_nt(qr, k_ref[pl.ds(k0, KR * GRID_W), :]) * NA_SCALE + bias_ref[base]
        s_ctx = _dot_nt(qr, kc) * NA_SCALE
        m = jnp.maximum(jnp.max(s_win, axis=-1, keepdims=True), jnp.max(s_ctx, axis=-1, keepdims=True))
        e_win = jnp.exp(s_win - m)
        e_ctx = jnp.exp(s_ctx - m)
        denom = jnp.sum(e_win, axis=-1, keepdims=True) + jnp.sum(e_ctx, axis=-1, keepdims=True)
        o = _dot(e_win.astype(BF16), v_ref[pl.ds(k0, KR * GRID_W), :]) + _dot(e_ctx.astype(BF16), vc)
        o_ref[pl.ds(q0, GRID_W), :] = (o / denom).astype(BF16)
        return 0

    lax.fori_loop(0, GRID_H, one_row, 0)

    if with_ctx_out:
        s = _dot_nt(qc_ref[...], kc) * NA_SCALE
        e = jnp.exp(s - jnp.max(s, axis=-1, keepdims=True))
        o = _dot(e.astype(BF16), vc)
        oc_ref[...] = (o / jnp.sum(e, axis=-1, keepdims=True)).astype(BF16)


def _natten(qk, vf, bias, with_ctx_out):
    lat = lambda col0: pl.BlockSpec((L, NA_DH), lambda h, b: (b, col0 + h))
    ctx = lambda col0: pl.BlockSpec((LC, NA_DH), lambda h, b: (T_LAT // LC + b, col0 + h))
    in_specs = [lat(0), lat(NA_HEADS), lat(0), ctx(NA_HEADS), ctx(0),
                pl.BlockSpec((None, N_BASE, GRID_W, KR * GRID_W), lambda h, b: (h, 0, 0, 0))]
    args = [qk, qk, vf, qk, vf, bias]
    out_shape = [jax.ShapeDtypeStruct((T_LAT, NA_W), BF16)]
    out_specs = [pl.BlockSpec((L, NA_DH), lambda h, b: (b, h))]
    if with_ctx_out:
        in_specs.append(ctx(0))
        args.append(qk)
        out_shape.append(jax.ShapeDtypeStruct((T_CTX, NA_W), BF16))
        out_specs.append(pl.BlockSpec((LC, NA_DH), lambda h, b: (b, h)))
    res = pl.pallas_call(
        functools.partial(_natten_kernel, with_ctx_out=with_ctx_out),
        out_shape=out_shape, grid=(NA_HEADS, B), in_specs=in_specs, out_specs=out_specs,
        compiler_params=_cparams(2), name="natten")(*args)
    return res if with_ctx_out else (res[0], None)


def _dft_table_kernel(o_ref, *, n, tr):
    k = lax.broadcasted_iota(jnp.int32, (tr, n), 0) + pl.program_id(0) * tr
    m = lax.broadcasted_iota(jnp.int32, (tr, n), 1)
    ang = ((k * m) & (n - 1)).astype(F32) * (2.0 * math.pi / n)
    o_ref[:, 0:n] = jnp.cos(ang).astype(BF16)
    o_ref[:, n:2 * n] = jnp.sin(ang).astype(BF16)


def _dft_table(n):
    tr = min(n, 256)
    return pl.pallas_call(
        functools.partial(_dft_table_kernel, n=n, tr=tr),
        out_shape=jax.ShapeDtypeStruct((n, 2 * n), BF16),
        grid=(n // tr,),
        out_specs=pl.BlockSpec((tr, 2 * n), lambda i: (i, 0)),
        compiler_params=_cparams(1), name="dft_table")()


FFT_TL = 256


def _fourier_kernel(x_ref, tseq_ref, tch_ref, o_ref, z_ref, *, n):
    @pl.when(pl.program_id(1) == 0)
    def _():
        tch = tch_ref[...]
        for g in range(FFT_GROUPS):
            cs = slice(g * FFT_CG, (g + 1) * FFT_CG)
            y = _dot(x_ref[:, cs], tch)
            z_ref[0:n, cs] = y[:, 0:FFT_CG].astype(BF16)
            z_ref[n:2 * n, cs] = (-y[:, FFT_CG:2 * FFT_CG]).astype(BF16)

    o_ref[...] = (_dot(tseq_ref[...], z_ref[...]) * ((n * FFT_CG) ** -0.5)).astype(BF16)


def _fourier(vf, t_seq, t_ch, n, row0):
    tl = min(FFT_TL, n)
    col_blk = NA_W // FFT_W
    return pl.pallas_call(
        functools.partial(_fourier_kernel, n=n),
        out_shape=jax.ShapeDtypeStruct((B * n, FFT_W), BF16),
        grid=(B, n // tl),
        in_specs=[pl.BlockSpec((n, FFT_W), lambda b, i: (row0 // n + b, col_blk)),
                  pl.BlockSpec((tl, 2 * n), lambda b, i: (i, 0)),
                  pl.BlockSpec((FFT_CG, 2 * FFT_CG), lambda b, i: (0, 0))],
        out_specs=pl.BlockSpec((tl, FFT_W), lambda b, i: (b * (n // tl) + i, 0)),
        scratch_shapes=[pltpu.VMEM((2 * n, FFT_W), BF16)],
        compiler_params=_cparams(2), name="fourier")(vf, t_seq, t_ch)


def kernel(x, c, ctx, c_ctx, w_ada, b_ada, norm_g, ffn1_w13, ffn1_w2, ffn2_w13, ffn2_w2, w_in, conv_w, conv_b,
           lru_wa, lru_ba, lru_wi, lru_bi, lru_lam, na_rpb, w_out_lru, w_out_na, w_out_fft, w_o, final_g):
    xa = jnp.concatenate([x.reshape(T_LAT, D), ctx.reshape(T_CTX, D)], axis=0)
    c8 = jnp.concatenate([c, c_ctx[None, :], jnp.zeros((MOD_ROWS - B - 1, D), F32)], axis=0)
    mod3 = _ada(c8, w_ada, b_ada).reshape(DEPTH * MOD_ROWS, 1, N_MOD * D)
    norm_g3 = norm_g.reshape(DEPTH * 3, 1, D)
    cos, sin = _rope_tables()
    rope_aux = _rope_aux(jnp.asarray(cos), jnp.asarray(sin))
    t_lat = _dft_table(L)
    t_ctx = _dft_table(LC)
    assert LC == FFT_CG

    for l in range(DEPTH):
        last = l == DEPTH - 1
        rows_tail = T_LAT if last else T_ALL
        h = _norm_mod(xa, norm_g3, mod3, l, 0, 0, 1, T_ALL)
        u = _swiglu_up(h, ffn1_w13, l, T_ALL)
        xa = _resid("ffn1_down", u, ffn1_w2, l, xa, mod3, 2, 0.5, tm=512, row0=0, n_rows=T_ALL)
        h = _norm_mod(xa, norm_g3, mod3, l, 1, 3, 4, T_ALL)
        u_lru = _in_proj("in_u", _plain_body, h, w_in, l, COL_U, LRU_W, F32, T_ALL)
        qk = _in_proj("in_qk", _rope_body, h, w_in, l, COL_Q, 2 * NA_W, BF16, T_ALL, rope_aux)
        vf = _in_proj("in_vf", _plain_body, h, w_in, l, COL_V, NA_W + FFT_W, BF16, T_ALL)
        gates = _in_proj("in_gates", _sigmoid_body, h, w_in, l, COL_G, 3 * D, F32, rows_tail)
        y_lru, y_lru_c = _lru(u_lru, conv_w, conv_b, lru_wa, lru_wi, lru_ba, lru_bi, lru_lam, l, not last)
        y_na, y_na_c = _natten(qk, vf, _bias_tables(na_rpb, l), not last)
        y_fft = _fourier(vf, t_lat, t_ctx, L, 0)
        m = _merge(y_lru, y_na, y_fft, gates, w_out_lru, w_out_na, w_out_fft, l, row0=0, n_rows=T_LAT)
        xa = _resid("mix_out", m, w_o, l, xa, mod3, 5, 1.0, tm=1024, row0=0, n_rows=T_LAT)
        if not last:
            y_fft_c = _fourier(vf, t_ctx, t_ctx, LC, T_LAT)
            mc = _merge(y_lru_c, y_na_c, y_fft_c, gates, w_out_lru, w_out_na, w_out_fft, l,
                        row0=T_LAT, n_rows=T_CTX)
            xa = _resid("mix_out_ctx", mc, w_o, l, xa, mod3, 5, 1.0, tm=1024, row0=T_LAT, n_rows=T_CTX)
        h = _norm_mod(xa, norm_g3, mod3, l, 2, 6, 7, rows_tail)
        u = _swiglu_up(h, ffn2_w13, l, rows_tail)
        xa = _resid("ffn2_down", u, ffn2_w2, l, xa, mod3, 8, 0.5, tm=512, row0=0, n_rows=rows_tail)
    return _final_norm(xa, final_g).reshape(B, L, D)
```

```python
import functools
import math

import numpy as np
import jax
import jax.numpy as jnp
from jax import lax
from jax.experimental import pallas as pl
from jax.experimental.pallas import tpu as pltpu

D = 4096
B = 4
L = 2048
DEPTH = 2
GRID_W = 64
GRID_H = L // GRID_W
LC = 256
LRU_W = D // 4
LRU_BW = 128
LRU_C = 8.0
CONV_W = 4
CONV_LEFT = CONV_W // 2
NA_DH = 128
NA_W = D // 2
NA_HEADS = NA_W // NA_DH
WIN_R = 8
WIN_C = 16
ROPE_BASE = 10000.0
FFT_W = D // 4
FFT_GROUPS = 4
FFT_CG = FFT_W // FFT_GROUPS
D_FF = 3 * D // 2
N_MOD = 9
EPS = 1e-6
T_LAT = B * L
T_CTX = B * LC
T_ALL = T_LAT + T_CTX
COL_U = 0
COL_Q = LRU_W
COL_V = LRU_W + 2 * NA_W
COL_G = LRU_W + 3 * NA_W + FFT_W

V7X_VMEM_LIMIT_BYTES = 56 * 1024 * 1024
SUBLANES = 8
LANES = 128
MOD_ROWS = 8

BF16 = jnp.bfloat16
F32 = jnp.float32


def _cparams(n_axes):
    return pltpu.CompilerParams(dimension_semantics=("arbitrary",) * n_axes,
                                vmem_limit_bytes=V7X_VMEM_LIMIT_BYTES)


def _row_group(row_tile, tile_rows):
    return jnp.minimum((row_tile * tile_rows) // L, B)


def _dot(a, b):
    return jnp.dot(a, b, preferred_element_type=F32)


def _dot_nt(a, b):
    return lax.dot_general(a, b, (((1,), (1,)), ((), ())), preferred_element_type=F32)


ADA_TN = 1024


def _ada_kernel(c_ref, w_ref, b_ref, o_ref):
    c = c_ref[...]
    s = (c * jax.nn.sigmoid(c)).astype(BF16)
    o_ref[...] = _dot(s, w_ref[...].astype(BF16)) + b_ref[...]


def _ada(c8, w_ada, b_ada):
    n = N_MOD * D
    return pl.pallas_call(
        _ada_kernel,
        out_shape=jax.ShapeDtypeStruct((DEPTH, MOD_ROWS, n), F32),
        grid=(DEPTH, n // ADA_TN),
        in_specs=[pl.BlockSpec((MOD_ROWS, D), lambda l, j: (0, 0)),
                  pl.BlockSpec((None, D, ADA_TN), lambda l, j: (l, 0, j)),
                  pl.BlockSpec((None, 1, ADA_TN), lambda l, j: (l, 0, j))],
        out_specs=pl.BlockSpec((None, MOD_ROWS, ADA_TN), lambda l, j: (l, 0, j)),
        compiler_params=_cparams(2), name="ada")(c8, w_ada, b_ada.reshape(DEPTH, 1, n))


NORM_TR = 256


def _norm_mod_kernel(x_ref, g_ref, shift_ref, scale_ref, o_ref):
    x = x_ref[...]
    y = x * lax.rsqrt(jnp.mean(x * x, axis=-1, keepdims=True) + EPS) * g_ref[...]
    o_ref[...] = (y * (1.0 + scale_ref[...]) + shift_ref[...]).astype(BF16)


def _norm_mod(xa, norm_g3, mod3, l, k_norm, k_shift, k_scale, n_rows):
    ni = n_rows // NORM_TR
    mod_spec = lambda k: pl.BlockSpec(
        (None, 1, D), lambda i: (l * MOD_ROWS + _row_group(i, NORM_TR), 0, k))
    return pl.pallas_call(
        _norm_mod_kernel,
        out_shape=jax.ShapeDtypeStruct((T_ALL, D), BF16),
        grid=(ni,),
        in_specs=[pl.BlockSpec((NORM_TR, D), lambda i: (i, 0)),
                  pl.BlockSpec((None, 1, D), lambda i: (l * 3 + k_norm, 0, 0)),
                  mod_spec(k_shift), mod_spec(k_scale)],
        out_specs=pl.BlockSpec((NORM_TR, D), lambda i: (i, 0)),
        compiler_params=_cparams(1), name="norm_mod")(xa, norm_g3, mod3, mod3)


def _final_norm_kernel(x_ref, g_ref, o_ref):
    x = x_ref[...]
    o_ref[...] = x * lax.rsqrt(jnp.mean(x * x, axis=-1, keepdims=True) + EPS) * g_ref[...]


def _final_norm(xa, final_g):
    return pl.pallas_call(
        _final_norm_kernel,
        out_shape=jax.ShapeDtypeStruct((T_LAT, D), F32),
        grid=(T_LAT // NORM_TR,),
        in_specs=[pl.BlockSpec((NORM_TR, D), lambda i: (i, 0)),
                  pl.BlockSpec((1, D), lambda i: (0, 0))],
        out_specs=pl.BlockSpec((NORM_TR, D), lambda i: (i, 0)),
        compiler_params=_cparams(1), name="final_norm")(xa, final_g.reshape(1, D))


def _mm_kernel(*refs, n_x, n_w, n_aux, n_out, body):
    x_refs = refs[:n_x]
    w_refs = refs[n_x:n_x + n_w]
    aux_refs = refs[n_x + n_w:n_x + n_w + n_aux]
    out_refs = refs[n_x + n_w + n_aux:n_x + n_w + n_aux + n_out]
    wbf_refs = refs[n_x + n_w + n_aux + n_out:]

    @pl.when(pl.program_id(1) == 0)
    def _():
        for w_ref, wbf_ref in zip(w_refs, wbf_refs):
            wbf_ref[...] = w_ref[...].astype(BF16)

    body(x_refs, wbf_refs, aux_refs, out_refs)


def _mm(name, body, *, xs, ws, auxs, outs, tm, tn, ni, nj, i_off=0, aliases=None):
    in_specs, args, scratch = [], [], []
    for x in xs:
        in_specs.append(pl.BlockSpec((tm, x.shape[1]), lambda j, i: (i + i_off, 0)))
        args.append(x)
    for w, layer, col0 in ws:
        k = w.shape[1]
        in_specs.append(pl.BlockSpec((None, k, tn), functools.partial(
            lambda j, i, layer, cb: (layer, 0, cb + j), layer=layer, cb=col0 // tn)))
        args.append(w)
        scratch.append(pltpu.VMEM((k, tn), BF16))
    for a, spec in auxs:
        in_specs.append(spec)
        args.append(a)
    kern = functools.partial(_mm_kernel, n_x=len(xs), n_w=len(ws), n_aux=len(auxs), n_out=len(outs),
                             body=body)
    res = pl.pallas_call(
        kern,
        out_shape=[o for o, _ in outs],
        grid=(nj, ni),
        in_specs=in_specs,
        out_specs=[s for _, s in outs],
        scratch_shapes=scratch,
        input_output_aliases=aliases or {},
        compiler_params=_cparams(2), name=name)(*args)
    return res


def _mod_tile_spec(l, k_mod, tm, tn, i_off):
    return pl.BlockSpec((None, 1, tn), lambda j, i: (
        l * MOD_ROWS + _row_group(i + i_off, tm), 0, k_mod * (D // tn) + j))


UP_TM, UP_TN = 1024, 256


def _swiglu_body(x_refs, w_refs, aux_refs, out_refs):
    h = x_refs[0][...]
    a = _dot(h, w_refs[0][...])
    b = _dot(h, w_refs[1][...])
    out_refs[0][...] = (a * jax.nn.sigmoid(a) * b).astype(BF16)


def _swiglu_up(h, w13, l, n_rows):
    out = jax.ShapeDtypeStruct((T_ALL, D_FF), BF16)
    return _mm("ffn_up", _swiglu_body, xs=[h], ws=[(w13, l, 0), (w13, l, D_FF)], auxs=[],
               outs=[(out, pl.BlockSpec((UP_TM, UP_TN), lambda j, i: (i, j)))],
               tm=UP_TM, tn=UP_TN, ni=n_rows // UP_TM, nj=D_FF // UP_TN)[0]


RES_TN = 512


def _resid_body(x_refs, w_refs, aux_refs, out_refs, *, coef):
    y = _dot(x_refs[0][...], w_refs[0][...])
    out_refs[0][...] = aux_refs[0][...] + (coef * aux_refs[1][...]) * y


def _resid(name, u, w, l, xa, mod3, k_gate, coef, *, tm, row0, n_rows):
    i_off = row0 // tm
    n_x_w = 2
    tile = pl.BlockSpec((tm, RES_TN), lambda j, i: (i + i_off, j))
    xs_spec_rows = u.shape[0]
    assert xs_spec_rows in (T_ALL, n_rows)
    u_off = i_off if xs_spec_rows == T_ALL else 0
    in_x = pl.BlockSpec((tm, u.shape[1]), lambda j, i: (i + u_off, 0))
    body = functools.partial(_resid_body, coef=coef)
    k = w.shape[1]
    kern = functools.partial(_mm_kernel, n_x=1, n_w=1, n_aux=2, n_out=1, body=body)
    return pl.pallas_call(
        kern,
        out_shape=jax.ShapeDtypeStruct((T_ALL, D), F32),
        grid=(D // RES_TN, n_rows // tm),
        in_specs=[in_x,
                  pl.BlockSpec((None, k, RES_TN), lambda j, i: (l, 0, j)),
                  tile,
                  _mod_tile_spec(l, k_gate, tm, RES_TN, i_off)],
        out_specs=tile,
        scratch_shapes=[pltpu.VMEM((k, RES_TN), BF16)],
        input_output_aliases={n_x_w: 0},
        compiler_params=_cparams(2), name=name)(u, w, xa, mod3)


IN_TM, IN_TN = 1024, 512


def _plain_body(x_refs, w_refs, aux_refs, out_refs):
    out_refs[0][...] = _dot(x_refs[0][...], w_refs[0][...]).astype(out_refs[0].dtype)


def _sigmoid_body(x_refs, w_refs, aux_refs, out_refs):
    out_refs[0][...] = jax.nn.sigmoid(_dot(x_refs[0][...], w_refs[0][...]))


def _rope_body(x_refs, w_refs, aux_refs, out_refs):
    acc = _dot(x_refs[0][...], w_refs[0][...])
    cos = aux_refs[0][...]
    sin = aux_refs[1][...]
    lane = lax.broadcasted_iota(jnp.int32, (IN_TM, NA_DH), 1)
    first_half = (lane % (NA_DH // 2)) < (NA_DH // 4)
    for hh in range(IN_TN // NA_DH):
        xh = acc[:, hh * NA_DH:(hh + 1) * NA_DH]
        partner = jnp.where(first_half,
                            pltpu.roll(xh, NA_DH - NA_DH // 4, 1),
                            pltpu.roll(xh, NA_DH // 4, 1))
        out_refs[0][:, hh * NA_DH:(hh + 1) * NA_DH] = (xh * cos + partner * sin).astype(BF16)


def _rope_tables():
    half = NA_DH // 2
    t = np.arange(L)
    inv = (ROPE_BASE ** (-np.arange(0, half, 2, dtype=np.float32) / np.float32(half))).astype(np.float32)
    ang_r = ((t // GRID_W).astype(np.float32)[:, None] * inv[None, :]).astype(np.float32)
    ang_c = ((t % GRID_W).astype(np.float32)[:, None] * inv[None, :]).astype(np.float32)
    cos = np.concatenate([np.cos(ang_r), np.cos(ang_r), np.cos(ang_c), np.cos(ang_c)], axis=1)
    sin = np.concatenate([-np.sin(ang_r), np.sin(ang_r), -np.sin(ang_c), np.sin(ang_c)], axis=1)
    cos = np.concatenate([cos, np.ones((IN_TM, NA_DH))], axis=0).astype(np.float32)
    sin = np.concatenate([sin, np.zeros((IN_TM, NA_DH))], axis=0).astype(np.float32)
    return cos, sin


def _in_proj(name, body, h, w_in, l, col0, width, dtype, n_rows, auxs=()):
    out = jax.ShapeDtypeStruct((T_ALL, width), dtype)
    return _mm(name, body, xs=[h], ws=[(w_in, l, col0)], auxs=list(auxs),
               outs=[(out, pl.BlockSpec((IN_TM, IN_TN), lambda j, i: (i, j)))],
               tm=IN_TM, tn=IN_TN, ni=n_rows // IN_TM, nj=width // IN_TN)[0]


def _rope_aux(cos, sin):
    lat_tiles = T_LAT // IN_TM
    per_seq = L // IN_TM
    spec = pl.BlockSpec((IN_TM, NA_DH), lambda j, i: (jnp.where(i < lat_tiles, i % per_seq, per_seq), 0))
    return [(cos, spec), (sin, spec)]


MERGE_TM, MERGE_TN = 512, 512


def _merge_body(x_refs, w_refs, aux_refs, out_refs):
    m = aux_refs[0][...] * _dot(x_refs[0][...], w_refs[0][...])
    m = m + aux_refs[1][...] * _dot(x_refs[1][...], w_refs[1][...])
    m = m + aux_refs[2][...] * _dot(x_refs[2][...], w_refs[2][...])
    out_refs[0][...] = m.astype(BF16)


def _merge(y_lru, y_na, y_fft, gates, w_out_lru, w_out_na, w_out_fft, l, *, row0, n_rows):
    tm = min(MERGE_TM, n_rows)
    i_off = row0 // tm
    gate_spec = lambda b: pl.BlockSpec((tm, MERGE_TN), lambda j, i: (i + i_off, b * (D // MERGE_TN) + j))
    out = jax.ShapeDtypeStruct((n_rows, D), BF16)
    return _mm("merge", _merge_body, xs=[y_lru, y_na, y_fft],
               ws=[(w_out_lru, l, 0), (w_out_na, l, 0), (w_out_fft, l, 0)],
               auxs=[(gates, gate_spec(0)), (gates, gate_spec(1)), (gates, gate_spec(2))],
               outs=[(out, pl.BlockSpec((tm, MERGE_TN), lambda j, i: (i, j)))],
               tm=tm, tn=MERGE_TN, ni=n_rows // tm, nj=D // MERGE_TN)[0]


LRU_CT = 256
LRU_CHUNK = 256
LRU_PAD = SUBLANES
LRU_T = LC + L


def _softplus(x):
    return jnp.maximum(x, 0.0) + jnp.log1p(jnp.exp(-jnp.abs(x)))


def _lru_kernel(ul_ref, uc_ref, cw_ref, cb_ref, wa_ref, wi_ref, ba_ref, bi_ref, lam_ref, *rest, with_ctx_out):
    if with_ctx_out:
        yl_ref, yc_ref, pad_l, pad_c, a_f, x_f, a_b, x_b = rest
    else:
        yl_ref, pad_l, pad_c, a_f, x_f, a_b, x_b = rest
        yc_ref = None
    zeros_pad = jnp.zeros((LRU_PAD, LRU_CT), F32)
    for pad, src, n in ((pad_l, ul_ref, L), (pad_c, uc_ref, LC)):
        pad[0:LRU_PAD, :] = zeros_pad
        pad[LRU_PAD + n:2 * LRU_PAD + n, :] = zeros_pad
        pad[LRU_PAD:LRU_PAD + n, :] = src[...]

    cw = cw_ref[...]
    cb = cb_ref[...]
    sp = _softplus(-lam_ref[...])
    sub = lax.broadcasted_iota(jnp.int32, (1, SUBLANES, LRU_BW), 1)
    grouped = (LRU_CHUNK // SUBLANES, SUBLANES, LRU_BW)

    def sigmoid(z):
        return 0.5 * jnp.tanh(0.5 * z) + 0.5

    def gate_terms(v, vb, blk, d):
        cs = slice(blk * LRU_BW, (blk + 1) * LRU_BW)
        r = sigmoid(_dot(vb, wa_ref[d, blk].astype(BF16)) + ba_ref[d:d + 1, cs])
        g = sigmoid(_dot(vb, wi_ref[d, blk].astype(BF16)) + bi_ref[d:d + 1, cs])
        log_a = (-LRU_C) * r * sp[d:d + 1, cs]
        a = jnp.exp(log_a)
        y = 1.0 - a * a
        return a, jnp.where(y > 0.0, y * lax.rsqrt(y), 0.0) * (g * v)

    def local_scan(a, x, forward):
        a = a.reshape(grouped)
        x = x.reshape(grouped)
        s = 1
        while s < SUBLANES:
            m = (sub >= s) if forward else (sub < SUBLANES - s)
            shift = s if forward else SUBLANES - s
            a_n = jnp.where(m, pltpu.roll(a, shift, 1), 1.0)
            x_n = jnp.where(m, pltpu.roll(x, shift, 1), 0.0)
            x = a * x_n + x
            a = a * a_n
            s *= 2
        return a.reshape(LRU_CHUNK, LRU_BW), x.reshape(LRU_CHUNK, LRU_BW)

    for pad, n, f_off, b_off in ((pad_c, LC, 0, L), (pad_l, L, LC, 0)):
        for r0 in range(0, n, LRU_CHUNK):
            v = cb
            for j in range(CONV_W):
                start = LRU_PAD + r0 + j - CONV_LEFT
                v = v + cw[j:j + 1, :] * pad[start:start + LRU_CHUNK, :]
            for blk in range(LRU_CT // LRU_BW):
                cs = slice(blk * LRU_BW, (blk + 1) * LRU_BW)
                vblk = v[:, cs]
                vb = vblk.astype(BF16)
                a, x = local_scan(*gate_terms(vblk, vb, blk, 0), True)
                a_f[f_off + r0:f_off + r0 + LRU_CHUNK, cs] = a
                x_f[f_off + r0:f_off + r0 + LRU_CHUNK, cs] = x
                a, x = local_scan(*gate_terms(vblk, vb, blk, 1), False)
                a_b[b_off + r0:b_off + r0 + LRU_CHUNK, cs] = a
                x_b[b_off + r0:b_off + r0 + LRU_CHUNK, cs] = x

    n_groups = LRU_T // SUBLANES

    def carry_step(g, carry):
        hf, hb = carry
        rf = pl.multiple_of(g * SUBLANES, SUBLANES)
        h = a_f[pl.ds(rf, SUBLANES), :] * hf + x_f[pl.ds(rf, SUBLANES), :]
        x_f[pl.ds(rf, SUBLANES), :] = h
        hf = h[SUBLANES - 1:SUBLANES, :]
        rb = pl.multiple_of((n_groups - 1 - g) * SUBLANES, SUBLANES)
        h = a_b[pl.ds(rb, SUBLANES), :] * hb + x_b[pl.ds(rb, SUBLANES), :]
        x_b[pl.ds(rb, SUBLANES), :] = h
        hb = h[0:1, :]
        return hf, hb

    zero_state = jnp.zeros((1, LRU_CT), F32)
    lax.fori_loop(0, n_groups, carry_step, (zero_state, zero_state))

    yl_ref[...] = (x_f[LC:LC + L, :] + x_b[0:L, :]).astype(BF16)
    if with_ctx_out:
        yc_ref[...] = (x_f[0:LC, :] + x_b[L:L + LC, :]).astype(BF16)


def _lru(u_all, conv_w, conv_b, lru_wa, lru_wi, lru_ba, lru_bi, lru_lam, l, with_ctx_out):
    nct = LRU_W // LRU_CT
    bpc = LRU_CT // LRU_BW
    vec = lambda rows: pl.BlockSpec((None, rows, LRU_CT), lambda b, c: (l, 0, c))
    wspec = pl.BlockSpec((None, 2, bpc, LRU_BW, LRU_BW), lambda b, c: (l, 0, c, 0, 0))
    out_shape = [jax.ShapeDtypeStruct((T_LAT, LRU_W), BF16)]
    out_specs = [pl.BlockSpec((L, LRU_CT), lambda b, c: (b, c))]
    if with_ctx_out:
        out_shape.append(jax.ShapeDtypeStruct((T_CTX, LRU_W), BF16))
        out_specs.append(pl.BlockSpec((LC, LRU_CT), lambda b, c: (b, c)))
    res = pl.pallas_call(
        functools.partial(_lru_kernel, with_ctx_out=with_ctx_out),
        out_shape=out_shape,
        grid=(B, nct),
        in_specs=[pl.BlockSpec((L, LRU_CT), lambda b, c: (b, c)),
                  pl.BlockSpec((LC, LRU_CT), lambda b, c: (T_LAT // LC + b, c)),
                  vec(CONV_W), vec(1), wspec, wspec, vec(2), vec(2), vec(2)],
        out_specs=out_specs,
        scratch_shapes=[pltpu.VMEM((L + 2 * LRU_PAD, LRU_CT), F32),
                        pltpu.VMEM((LC + 2 * LRU_PAD, LRU_CT), F32)]
                       + [pltpu.VMEM((LRU_T, LRU_CT), F32)] * 4,
        compiler_params=_cparams(2), name="rglru")(
            u_all, u_all, conv_w, conv_b.reshape(DEPTH, 1, LRU_W), lru_wa, lru_wi, lru_ba, lru_bi, lru_lam)
    return res if with_ctx_out else (res[0], None)


KR = min(WIN_R, GRID_H)
N_BASE = WIN_R
NA_SCALE = NA_DH ** -0.5


def _bias_kernel(rpb_ref, o_ref):
    q = lax.broadcasted_iota(jnp.int32, (GRID_W, LANES), 0)
    lane = lax.broadcasted_iota(jnp.int32, (GRID_W, LANES), 1)
    kcol = lane % GRID_W
    col_start = jnp.clip(q - WIN_C // 2, 0, GRID_W - WIN_C)
    col_in = (kcol >= col_start) & (kcol < col_start + WIN_C)
    tiles = []
    for ri in range(2 * WIN_R - 1):
        rowv = jnp.broadcast_to(rpb_ref[ri:ri + 1, :], (GRID_W, LANES))
        lo = pltpu.roll(rowv, LANES - (WIN_C - 1), 1, stride=1, stride_axis=0)
        hi = pltpu.roll(rowv, GRID_W - (WIN_C - 1), 1, stride=1, stride_axis=0)
        t = jnp.where(lane < GRID_W, lo, hi)
        tiles.append(jnp.where(col_in, t, -jnp.inf))
    for base in range(N_BASE):
        for p in range(KR // 2):
            t = jnp.where(lane < GRID_W, tiles[base + 2 * p], tiles[base + 2 * p + 1])
            o_ref[base, :, p * LANES:(p + 1) * LANES] = t


def _bias_tables(na_rpb, l):
    pad = LANES - (2 * WIN_C - 1)
    rpb = jnp.pad(na_rpb, ((0, 0), (0, 0), (0, 0), (0, pad)))
    return pl.pallas_call(
        _bias_kernel,
        out_shape=jax.ShapeDtypeStruct((NA_HEADS, N_BASE, GRID_W, KR * GRID_W), F32),
        grid=(NA_HEADS,),
        in_specs=[pl.BlockSpec((None, None, 2 * WIN_R - 1, LANES), lambda h: (l, h, 0, 0))],
        out_specs=pl.BlockSpec((None, N_BASE, GRID_W, KR * GRID_W), lambda h: (h, 0, 0, 0)),
        compiler_params=_cparams(1), name="na_bias")(rpb)


NA_NWIN = KR * GRID_W
NA_NKEYS = NA_NWIN + LC
NA_CTX_ROWS = 128
NA_UNROLL = 4


def _natten_kernel(q_ref, k_ref, v_ref, kc_ref, vc_ref, bias_ref, *rest, with_ctx_out):
    if with_ctx_out:
        qc_ref, o_ref, oc_ref, s_ref, m_ref, pc_ref, acc_ref, vx_ref, vcx_ref = rest
    else:
        o_ref, s_ref, m_ref, pc_ref, acc_ref, vx_ref, vcx_ref = rest
    kc = kc_ref[...]
    vc = vc_ref[...]
    vx_ref[:, 0:NA_DH] = v_ref[...]
    vx_ref[:, NA_DH:] = jnp.ones((L, NA_DH), BF16)
    vcx_ref[:, 0:NA_DH] = vc
    vcx_ref[:, NA_DH:] = jnp.ones((LC, NA_DH), BF16)

    def row_max(s):
        return jnp.broadcast_to(jnp.max(s, axis=-1, keepdims=True), (s.shape[0], NA_DH))

    def row_offsets(r):
        rs = jnp.clip(r - KR // 2, 0, GRID_H - KR)
        return (rs - r + WIN_R - 1, pl.multiple_of(r * GRID_W, GRID_W), pl.multiple_of(rs * GRID_W, GRID_W))

    def ctx_scores(c, _):
        r0 = pl.multiple_of(c * NA_CTX_ROWS, NA_CTX_ROWS)
        s = _dot_nt(q_ref[pl.ds(r0, NA_CTX_ROWS), :], kc) * NA_SCALE
        s_ref[pl.ds(r0, NA_CTX_ROWS), NA_NWIN:] = s
        m_ref[pl.ds(r0, NA_CTX_ROWS), :] = row_max(s)
        return 0

    lax.fori_loop(0, L // NA_CTX_ROWS, ctx_scores, 0, unroll=2)

    def win_scores(r, _):
        base, q0, k0 = row_offsets(r)
        s = _dot_nt(q_ref[pl.ds(q0, GRID_W), :], k_ref[pl.ds(k0, NA_NWIN), :]) * NA_SCALE + bias_ref[base]
        s_ref[pl.ds(q0, GRID_W), 0:NA_NWIN] = s
        m_ref[pl.ds(q0, GRID_W), :] = jnp.maximum(m_ref[pl.ds(q0, GRID_W), :], row_max(s))
        return 0

    lax.fori_loop(0, GRID_H, win_scores, 0, unroll=NA_UNROLL)

    def ctx_probs(c, _):
        r0 = pl.multiple_of(c * NA_CTX_ROWS, NA_CTX_ROWS)
        m = m_ref[pl.ds(r0, NA_CTX_ROWS), :]
        for t in range(LC // NA_DH):
            cols = slice(NA_NWIN + t * NA_DH, NA_NWIN + (t + 1) * NA_DH)
            pc_ref[pl.ds(r0, NA_CTX_ROWS), t * NA_DH:(t + 1) * NA_DH] = jnp.exp(
                s_ref[pl.ds(r0, NA_CTX_ROWS), cols] - m).astype(BF16)
        return 0

    lax.fori_loop(0, L // NA_CTX_ROWS, ctx_probs, 0, unroll=2)
    acc_ref[...] = _dot(pc_ref[...], vcx_ref[...])

    def win_out(r, _):
        _, q0, k0 = row_offsets(r)
        m = m_ref[pl.ds(q0, GRID_W), :]
        p = jnp.concatenate(
            [jnp.exp(s_ref[pl.ds(q0, GRID_W), t * NA_DH:(t + 1) * NA_DH] - m).astype(BF16)
             for t in range(NA_NWIN // NA_DH)], axis=1)
        o = _dot(p, vx_ref[pl.ds(k0, NA_NWIN), :]) + acc_ref[pl.ds(q0, GRID_W), :]
        o_ref[pl.ds(q0, GRID_W), :] = (o[:, 0:NA_DH] * (1.0 / o[:, NA_DH:])).astype(BF16)
        return 0

    lax.fori_loop(0, GRID_H, win_out, 0, unroll=NA_UNROLL)

    if with_ctx_out:
        s = _dot_nt(qc_ref[...], kc) * NA_SCALE
        e = jnp.exp(s - jnp.max(s, axis=-1, keepdims=True))
        p = (e * (1.0 / jnp.sum(e, axis=-1, keepdims=True))).astype(BF16)
        oc_ref[...] = _dot(p, vc).astype(BF16)


def _natten(qk, vf, bias, with_ctx_out):
    lat = lambda col0: pl.BlockSpec((L, NA_DH), lambda h, b: (b, col0 + h))
    ctx = lambda col0: pl.BlockSpec((LC, NA_DH), lambda h, b: (T_LAT // LC + b, col0 + h))
    in_specs = [lat(0), lat(NA_HEADS), lat(0), ctx(NA_HEADS), ctx(0),
                pl.BlockSpec((None, N_BASE, GRID_W, KR * GRID_W), lambda h, b: (h, 0, 0, 0))]
    args = [qk, qk, vf, qk, vf, bias]
    out_shape = [jax.ShapeDtypeStruct((T_LAT, NA_W), BF16)]
    out_specs = [pl.BlockSpec((L, NA_DH), lambda h, b: (b, h))]
    if with_ctx_out:
        in_specs.append(ctx(0))
        args.append(qk)
        out_shape.append(jax.ShapeDtypeStruct((T_CTX, NA_W), BF16))
        out_specs.append(pl.BlockSpec((LC, NA_DH), lambda h, b: (b, h)))
    res = pl.pallas_call(
        functools.partial(_natten_kernel, with_ctx_out=with_ctx_out),
        out_shape=out_shape, grid=(NA_HEADS, B), in_specs=in_specs, out_specs=out_specs,
        scratch_shapes=[pltpu.VMEM((L, NA_NKEYS), F32),
                        pltpu.VMEM((L, NA_DH), F32),
                        pltpu.VMEM((L, LC), BF16),
                        pltpu.VMEM((L, 2 * NA_DH), F32),
                        pltpu.VMEM((L, 2 * NA_DH), BF16),
                        pltpu.VMEM((LC, 2 * NA_DH), BF16)],
        compiler_params=_cparams(2), name="natten")(*args)
    return res if with_ctx_out else (res[0], None)


def _dft_table_kernel(o_ref, *, n, tr):
    k = lax.broadcasted_iota(jnp.int32, (tr, n), 0) + pl.program_id(0) * tr
    m = lax.broadcasted_iota(jnp.int32, (tr, n), 1)
    ang = ((k * m) & (n - 1)).astype(F32) * (2.0 * math.pi / n)
    o_ref[:, 0:n] = jnp.cos(ang).astype(BF16)
    o_ref[:, n:2 * n] = jnp.sin(ang).astype(BF16)


def _dft_table(n):
    tr = min(n, 256)
    return pl.pallas_call(
        functools.partial(_dft_table_kernel, n=n, tr=tr),
        out_shape=jax.ShapeDtypeStruct((n, 2 * n), BF16),
        grid=(n // tr,),
        out_specs=pl.BlockSpec((tr, 2 * n), lambda i: (i, 0)),
        compiler_params=_cparams(1), name="dft_table")()


FFT_TL = 256


def _fourier_kernel(x_ref, tseq_ref, tch_ref, o_ref, z_ref, *, n):
    @pl.when(pl.program_id(1) == 0)
    def _():
        tch = tch_ref[...]
        for g in range(FFT_GROUPS):
            cs = slice(g * FFT_CG, (g + 1) * FFT_CG)
            y = _dot(x_ref[:, cs], tch)
            z_ref[0:n, cs] = y[:, 0:FFT_CG].astype(BF16)
            z_ref[n:2 * n, cs] = (-y[:, FFT_CG:2 * FFT_CG]).astype(BF16)

    o_ref[...] = (_dot(tseq_ref[...], z_ref[...]) * ((n * FFT_CG) ** -0.5)).astype(BF16)


def _fourier(vf, t_seq, t_ch, n, row0):
    tl = min(FFT_TL, n)
    col_blk = NA_W // FFT_W
    return pl.pallas_call(
        functools.partial(_fourier_kernel, n=n),
        out_shape=jax.ShapeDtypeStruct((B * n, FFT_W), BF16),
        grid=(B, n // tl),
        in_specs=[pl.BlockSpec((n, FFT_W), lambda b, i: (row0 // n + b, col_blk)),
                  pl.BlockSpec((tl, 2 * n), lambda b, i: (i, 0)),
                  pl.BlockSpec((FFT_CG, 2 * FFT_CG), lambda b, i: (0, 0))],
        out_specs=pl.BlockSpec((tl, FFT_W), lambda b, i: (b * (n // tl) + i, 0)),
        scratch_shapes=[pltpu.VMEM((2 * n, FFT_W), BF16)],
        compiler_params=_cparams(2), name="fourier")(vf, t_seq, t_ch)


def kernel(x, c, ctx, c_ctx, w_ada, b_ada, norm_g, ffn1_w13, ffn1_w2, ffn2_w13, ffn2_w2, w_in, conv_w, conv_b,
           lru_wa, lru_ba, lru_wi, lru_bi, lru_lam, na_rpb, w_out_lru, w_out_na, w_out_fft, w_o, final_g):
    xa = jnp.concatenate([x.reshape(T_LAT, D), ctx.reshape(T_CTX, D)], axis=0)
    c8 = jnp.concatenate([c, c_ctx[None, :], jnp.zeros((MOD_ROWS - B - 1, D), F32)], axis=0)
    mod3 = _ada(c8, w_ada, b_ada).reshape(DEPTH * MOD_ROWS, 1, N_MOD * D)
    norm_g3 = norm_g.reshape(DEPTH * 3, 1, D)
    cos, sin = _rope_tables()
    rope_aux = _rope_aux(jnp.asarray(cos), jnp.asarray(sin))
    t_lat = _dft_table(L)
    t_ctx = _dft_table(LC)
    assert LC == FFT_CG

    for l in range(DEPTH):
        last = l == DEPTH - 1
        rows_tail = T_LAT if last else T_ALL
        h = _norm_mod(xa, norm_g3, mod3, l, 0, 0, 1, T_ALL)
        u = _swiglu_up(h, ffn1_w13, l, T_ALL)
        xa = _resid("ffn1_down", u, ffn1_w2, l, xa, mod3, 2, 0.5, tm=512, row0=0, n_rows=T_ALL)
        h = _norm_mod(xa, norm_g3, mod3, l, 1, 3, 4, T_ALL)
        u_lru = _in_proj("in_u", _plain_body, h, w_in, l, COL_U, LRU_W, F32, T_ALL)
        qk = _in_proj("in_qk", _rope_body, h, w_in, l, COL_Q, 2 * NA_W, BF16, T_ALL, rope_aux)
        vf = _in_proj("in_vf", _plain_body, h, w_in, l, COL_V, NA_W + FFT_W, BF16, T_ALL)
        gates = _in_proj("in_gates", _sigmoid_body, h, w_in, l, COL_G, 3 * D, F32, rows_tail)
        y_lru, y_lru_c = _lru(u_lru, conv_w, conv_b, lru_wa, lru_wi, lru_ba, lru_bi, lru_lam, l, not last)
        y_na, y_na_c = _natten(qk, vf, _bias_tables(na_rpb, l), not last)
        y_fft = _fourier(vf, t_lat, t_ctx, L, 0)
        m = _merge(y_lru, y_na, y_fft, gates, w_out_lru, w_out_na, w_out_fft, l, row0=0, n_rows=T_LAT)
        xa = _resid("mix_out", m, w_o, l, xa, mod3, 5, 1.0, tm=1024, row0=0, n_rows=T_LAT)
        if not last:
            y_fft_c = _fourier(vf, t_ctx, t_ctx, LC, T_LAT)
            mc = _merge(y_lru_c, y_na_c, y_fft_c, gates, w_out_lru, w_out_na, w_out_fft, l,
                        row0=T_LAT, n_rows=T_CTX)
            xa = _resid("mix_out_ctx", mc, w_o, l, xa, mod3, 5, 1.0, tm=1024, row0=T_LAT, n_rows=T_CTX)
        h = _norm_mod(xa, norm_g3, mod3, l, 2, 6, 7, rows_tail)
        u = _swiglu_up(h, ffn2_w13, l, rows_tail)
        xa = _resid("ffn2_down", u, ffn2_w2, l, xa, mod3, 8, 0.5, tm=512, row0=0, n_rows=rows_tail)
    return _final_norm(xa, final_g).reshape(B, L, D)
```
